```python
import jax, jax.numpy as jnp
from jax import lax
import numpy as np

D_MODEL = 1024
BATCH = 8
SEQ = 4096
DEPTH = 1

CHUNK = 64
N_MEM = 256
D_MIX = D_MODEL
D_CONV = 3 * D_MIX // 8
D_POOL = 3 * D_MIX // 8
D_ATT = D_MIX - D_CONV - D_POOL
N_MEM_HEADS = 4
HEAD_DIM = D_ATT // N_MEM_HEADS
CONV_WIDTH = 3
POOL_WINDOWS = (2, 4, 8, 16)
N_POOL_GROUPS = len(POOL_WINDOWS)
POOL_GROUP = D_POOL // N_POOL_GROUPS
D_IN_PROJ = 4 * D_CONV + 2 * D_POOL + 2 * D_ATT
EPS = 1e-6

kernel_name = "hybrid_conv_pool_memattn_block"


def rmsnorm(x, w):
    xf = x.astype(jnp.float32)
    y = xf * lax.rsqrt(jnp.mean(xf * xf, axis=-1, keepdims=True) + EPS)
    return (y * w.astype(jnp.float32)).astype(x.dtype)


def short_conv_causal(u, w):
    s = u.shape[1]
    up = jnp.pad(u, ((0, 0), (CONV_WIDTH - 1, 0), (0, 0)))
    y = up[:, 0:s] * w[0]
    for k in range(1, CONV_WIDTH):
        y = y + up[:, k:k + s] * w[k]
    return y


def multiscale_pool(u, pool_w, pool_scale):
    b, s, _ = u.shape
    ug = u.reshape(b, s, N_POOL_GROUPS, POOL_GROUP)
    ugf = ug.astype(jnp.float32)
    cs = jnp.cumsum(ugf, axis=1)
    t = jnp.arange(s)
    outs = []
    for g, win in enumerate(POOL_WINDOWS):
        c = cs[:, :, g]
        shifted = jnp.pad(c, ((0, 0), (win, 0), (0, 0)))[:, :s]
        cnt = jnp.minimum(t + 1, win).astype(jnp.float32)[:, None]
        outs.append((c - shifted) / cnt)
    pooled = jnp.stack(outs, axis=2)
    diff = (pooled - ugf).astype(u.dtype)
    mixed = jnp.einsum('bsgc,gcd->bsgd', diff, pool_w)
    return mixed.reshape(b, s, D_POOL) * pool_scale


def memory_attention(q, mem_n, w_kv):
    b, s, _ = q.shape
    m = mem_n.shape[1]
    kv = mem_n @ w_kv
    k, v = jnp.split(kv, 2, axis=-1)
    qh = q.reshape(b, s, N_MEM_HEADS, HEAD_DIM)
    kh = k.reshape(b, m, N_MEM_HEADS, HEAD_DIM)
    vh = v.reshape(b, m, N_MEM_HEADS, HEAD_DIM)
    scores = jnp.einsum('bshd,bmhd->bhsm', qh, kh).astype(jnp.float32) * (HEAD_DIM ** -0.5)
    probs = jax.nn.softmax(scores, axis=-1).astype(q.dtype)
    out = jnp.einsum('bhsm,bmhd->bshd', probs, vh)
    return out.reshape(b, s, D_ATT)


def hybrid_layer(x, mem, pre_w, mem_norm_w, w_in, conv_w, pool_w, pool_scale, w_kv, w_out, post_w):
    h = rmsnorm(x, pre_w)
    proj = h @ w_in
    splits = np.cumsum([D_CONV, D_CONV, D_CONV, D_CONV, D_POOL, D_POOL, D_ATT]).tolist()
    xc, bg, cg, gc, xp, gp, q, ga = jnp.split(proj, splits, axis=-1)
    y_conv = bg * short_conv_causal(cg * xc, conv_w)
    y_pool = multiscale_pool(xp, pool_w, pool_scale)
    y_att = memory_attention(q, rmsnorm(mem, mem_norm_w), w_kv)
    y = jnp.concatenate([y_conv * jax.nn.silu(gc),
                         y_pool * jax.nn.silu(gp),
                         y_att * jax.nn.silu(ga)], axis=-1) @ w_out
    return x + rmsnorm(y, post_w)


def setup_inputs(seed: int = 0) -> dict:
    key = jax.random.key(seed)
    ks = jax.random.split(key, 12)
    f32 = jnp.float32
    x = jax.random.normal(ks[0], (BATCH, SEQ, D_MODEL), f32)
    mem = jax.random.normal(ks[1], (BATCH, N_MEM, D_MODEL), f32)
    pre_norm_w = 1.0 + 0.02 * jax.random.normal(ks[2], (DEPTH, D_MODEL), f32)
    mem_norm_w = 1.0 + 0.02 * jax.random.normal(ks[3], (DEPTH, D_MODEL), f32)
    w_in = jax.random.normal(ks[4], (DEPTH, D_MODEL, D_IN_PROJ), f32) * D_MODEL ** -0.5
    conv_w = jax.random.normal(ks[5], (DEPTH, CONV_WIDTH, D_CONV), f32) * CONV_WIDTH ** -0.5
    pool_w = jax.random.normal(ks[6], (DEPTH, N_POOL_GROUPS, POOL_GROUP, POOL_GROUP), f32) * POOL_GROUP ** -0.5
    pool_scale = 1.0 + 0.02 * jax.random.normal(ks[7], (DEPTH, D_POOL), f32)
    w_kv = jax.random.normal(ks[8], (DEPTH, D_MODEL, 2 * D_ATT), f32) * D_MODEL ** -0.5
    w_out = jax.random.normal(ks[9], (DEPTH, D_MIX, D_MODEL), f32) * D_MIX ** -0.5
    post_norm_w = 1.0 + 0.02 * jax.random.normal(ks[10], (DEPTH, D_MODEL), f32)
    return {"x": x, "mem": mem, "pre_norm_w": pre_norm_w, "mem_norm_w": mem_norm_w,
            "w_in": w_in, "conv_w": conv_w, "pool_w": pool_w, "pool_scale": pool_scale,
            "w_kv": w_kv, "w_out": w_out, "post_norm_w": post_norm_w}


def reference(x, mem, pre_norm_w, mem_norm_w, w_in, conv_w, pool_w, pool_scale, w_kv, w_out, post_norm_w):
    for l in range(DEPTH):
        x = hybrid_layer(x, mem, pre_norm_w[l], mem_norm_w[l], w_in[l], conv_w[l], pool_w[l],
                         pool_scale[l], w_kv[l], w_out[l], post_norm_w[l])
    return x
```

```python
import functools

import jax
import jax.numpy as jnp
from jax import lax
from jax.experimental import pallas as pl
from jax.experimental.pallas import tpu as pltpu

D_MODEL = 1024
N_MEM = 256
D_CONV = 384
D_POOL = 384
D_ATT = 256
N_MEM_HEADS = 4
HEAD_DIM = D_ATT // N_MEM_HEADS
CONV_WIDTH = 3
POOL_WINDOWS = (2, 4, 8, 16)
POOL_GROUP = D_POOL // len(POOL_WINDOWS)
D_IN_PROJ = 4 * D_CONV + 2 * D_POOL + 2 * D_ATT
EPS = 1e-6

_C_CONV = 0
_C_POOL = 4 * D_CONV
_C_ATT = _C_POOL + 2 * D_POOL

_LANES = 128
_SUBLANES = 8

_TILE_T = 512
_CONV_HALO = _SUBLANES
_POOL_HALO = 32
_VMEM_LIMIT_BYTES = 56 * 1024 * 1024

_BF16 = jnp.bfloat16
_F32 = jnp.float32


def _rms_scale(v):
    return lax.rsqrt(jnp.mean(v * v, axis=-1, keepdims=True) + EPS)


def _silu(v):
    return v * jax.nn.sigmoid(v)


def _kv_kernel(mem_ref, mw_ref, wkv_ref, kbdt_ref, vbd_ref):
    m = mem_ref[0]
    mn = (m * _rms_scale(m) * mw_ref[...]).astype(_BF16)
    kv = jnp.dot(mn, wkv_ref[...], preferred_element_type=_F32)
    k = kv[:, :D_ATT] * (HEAD_DIM ** -0.5)
    v = kv[:, D_ATT:]
    kt = k.T
    kt4 = jnp.concatenate([kt] * N_MEM_HEADS, axis=1)
    r = lax.broadcasted_iota(jnp.int32, kt4.shape, 0) // HEAD_DIM
    c = lax.broadcasted_iota(jnp.int32, kt4.shape, 1) // N_MEM
    kbdt_ref[0] = jnp.where(r == c, kt4, 0.0).astype(_BF16)
    v4 = jnp.concatenate([v] * N_MEM_HEADS, axis=0)
    r = lax.broadcasted_iota(jnp.int32, v4.shape, 0) // N_MEM
    c = lax.broadcasted_iota(jnp.int32, v4.shape, 1) // HEAD_DIM
    vbd_ref[0] = jnp.where(r == c, v4, 0.0).astype(_BF16)


def _main_kernel(x_ref, prew_ref, win_ref, convw_ref, poolw_ref, pscale_ref,
                 kbdt_ref, vbd_ref, wout_ref, postw_ref, o_ref,
                 ubuf, xbuf, b2, b4, b8, dbuf, ycat):
    t_rows = x_ref.shape[1]
    s = pl.program_id(1)
    ch, ph = _CONV_HALO, _POOL_HALO

    @pl.when(s == 0)
    def _():
        ubuf[0:ch, :] = jnp.zeros((ch, D_CONV), _F32)
        xbuf[0:ph, :] = jnp.zeros((ph, D_POOL), _F32)

    x = x_ref[0]
    h = (x * _rms_scale(x) * prew_ref[...]).astype(_BF16)

    pc = jnp.dot(h, win_ref[:, _C_CONV:_C_POOL], preferred_element_type=_F32)
    xc = pc[:, 0:D_CONV]
    bg = pc[:, D_CONV:2 * D_CONV]
    cg = pc[:, 2 * D_CONV:3 * D_CONV]
    gc = pc[:, 3 * D_CONV:4 * D_CONV]
    u = cg * xc
    ubuf[ch:ch + t_rows, :] = u
    conv = u * convw_ref[CONV_WIDTH - 1:CONV_WIDTH, :]
    for k in range(CONV_WIDTH - 1):
        back = CONV_WIDTH - 1 - k
        conv = conv + ubuf[ch - back:ch - back + t_rows, :] * convw_ref[k:k + 1, :]
    ubuf[0:ch, :] = ubuf[t_rows:t_rows + ch, :]
    ycat[:, 0:D_CONV] = (bg * conv * _silu(gc)).astype(_BF16)

    pp = jnp.dot(h, win_ref[:, _C_POOL:_C_ATT], preferred_element_type=_F32)
    xp = pp[:, 0:D_POOL]
    gp = pp[:, D_POOL:2 * D_POOL]
    xbuf[ph:ph + t_rows, :] = xp
    n = t_rows + ph
    b2[8:n, :] = xbuf[8:n, :] + xbuf[7:n - 1, :]
    b4[16:n, :] = b2[16:n, :] + b2[14:n - 2, :]
    b8[24:n, 0:2 * _LANES] = b4[24:n, _LANES:] + b4[20:n - 4, _LANES:]
    s16 = b8[ph:n, _LANES:] + b8[ph - 8:n - 8, _LANES:]
    lane = lax.broadcasted_iota(jnp.int32, (1, _LANES), 1)
    w0 = jnp.where(lane < POOL_GROUP, b2[ph:n, 0:_LANES], b4[ph:n, 0:_LANES])
    w1 = jnp.where(lane < 2 * POOL_GROUP - _LANES, b4[ph:n, _LANES:2 * _LANES], b8[ph:n, 0:_LANES])
    w2 = jnp.where(lane < 3 * POOL_GROUP - 2 * _LANES, b8[ph:n, _LANES:], s16)
    wsum = jnp.concatenate([w0, w1, w2], axis=1)
    chan = lax.broadcasted_iota(jnp.int32, (1, D_POOL), 1)
    win = jnp.full((1, D_POOL), POOL_WINDOWS[-1], jnp.int32)
    for g in reversed(range(len(POOL_WINDOWS) - 1)):
        win = jnp.where(chan < (g + 1) * POOL_GROUP, POOL_WINDOWS[g], win)
    inv_win = 1.0 / win.astype(_F32)
    dbuf[...] = (wsum * inv_win - xp).astype(_BF16)

    @pl.when(s == 0)
    def _():
        top = POOL_WINDOWS[-1]
        trow = lax.broadcasted_iota(jnp.int32, (top, D_POOL), 0)
        cnt = jnp.minimum(trow + 1, win).astype(_F32)
        dbuf[0:top, :] = (wsum[0:top, :] / cnt - xp[0:top, :]).astype(_BF16)

    xbuf[0:ph, :] = xbuf[t_rows:t_rows + ph, :]
    mixed = jnp.dot(dbuf[...], poolw_ref[...], preferred_element_type=_F32)
    ycat[:, D_CONV:D_CONV + D_POOL] = (mixed * pscale_ref[...] * _silu(gp)).astype(_BF16)

    pa = jnp.dot(h, win_ref[:, _C_ATT:D_IN_PROJ], preferred_element_type=_F32)
    q = pa[:, 0:D_ATT].astype(_BF16)
    ga = pa[:, D_ATT:2 * D_ATT]
    sc = jnp.dot(q, kbdt_ref[0], preferred_element_type=_F32)
    probs = []
    for hd in range(N_MEM_HEADS):
        sh = sc[:, hd * N_MEM:(hd + 1) * N_MEM]
        e = jnp.exp(sh - jnp.max(sh, axis=-1, keepdims=True))
        probs.append((e * (1.0 / jnp.sum(e, axis=-1, keepdims=True))).astype(_BF16))
    p = jnp.concatenate(probs, axis=1)
    oa = jnp.dot(p, vbd_ref[0], preferred_element_type=_F32)
    ycat[:, D_CONV + D_POOL:D_MODEL] = (oa * _silu(ga)).astype(_BF16)

    y = jnp.dot(ycat[...], wout_ref[...], preferred_element_type=_F32)
    o_ref[0] = x_ref[0] + y * _rms_scale(y) * postw_ref[...]


def _block_diag(pw):
    g, a, b = pw.shape
    out = jnp.zeros((g * a, g * b), pw.dtype)
    for i in range(g):
        out = out.at[i * a:(i + 1) * a, i * b:(i + 1) * b].set(pw[i])
    return out


def _layer(x, mem, pre_w, mem_w, w_in, conv_w, pool_w, pool_scale, w_kv, w_out, post_w):
    b, s, d = x.shape
    m = mem.shape[1]
    assert d == D_MODEL and m == N_MEM and s % _TILE_T == 0
    hm = N_MEM_HEADS * N_MEM
    row = lambda v: v.reshape(1, -1)

    kbdt, vbd = pl.pallas_call(
        _kv_kernel,
        grid=(b,),
        in_specs=[
            pl.BlockSpec((1, m, d), lambda i: (i, 0, 0)),
            pl.BlockSpec((1, d), lambda i: (0, 0)),
            pl.BlockSpec((d, 2 * D_ATT), lambda i: (0, 0)),
        ],
        out_specs=[
            pl.BlockSpec((1, D_ATT, hm), lambda i: (i, 0, 0)),
            pl.BlockSpec((1, hm, D_ATT), lambda i: (i, 0, 0)),
        ],
        out_shape=[
            jax.ShapeDtypeStruct((b, D_ATT, hm), _BF16),
            jax.ShapeDtypeStruct((b, hm, D_ATT), _BF16),
        ],
        compiler_params=pltpu.CompilerParams(dimension_semantics=("arbitrary",)),
        name="kv_prep",
    )(mem, row(mem_w), w_kv.astype(_BF16))

    t = _TILE_T
    const = lambda shape: pl.BlockSpec(shape, lambda i, j: (0,) * len(shape))
    return pl.pallas_call(
        _main_kernel,
        grid=(b, s // t),
        in_specs=[
            pl.BlockSpec((1, t, d), lambda i, j: (i, j, 0)),
            const((1, d)),
            const((d, D_IN_PROJ)),
            const((CONV_WIDTH, D_CONV)),
            const((D_POOL, D_POOL)),
            const((1, D_POOL)),
            pl.BlockSpec((1, D_ATT, hm), lambda i, j: (i, 0, 0)),
            pl.BlockSpec((1, hm, D_ATT), lambda i, j: (i, 0, 0)),
            const((d, d)),
            const((1, d)),
        ],
        out_specs=pl.BlockSpec((1, t, d), lambda i, j: (i, j, 0)),
        out_shape=jax.ShapeDtypeStruct((b, s, d), x.dtype),
        scratch_shapes=[
            pltpu.VMEM((_CONV_HALO + t, D_CONV), _F32),
            pltpu.VMEM((_POOL_HALO + t, D_POOL), _F32),
            pltpu.VMEM((_POOL_HALO + t, D_POOL), _F32),
            pltpu.VMEM((_POOL_HALO + t, D_POOL), _F32),
            pltpu.VMEM((_POOL_HALO + t, 2 * _LANES), _F32),
            pltpu.VMEM((t, D_POOL), _BF16),
            pltpu.VMEM((t, D_MODEL), _BF16),
        ],
        compiler_params=pltpu.CompilerParams(
            dimension_semantics=("arbitrary", "arbitrary"),
            vmem_limit_bytes=_VMEM_LIMIT_BYTES,
        ),
        name="hybrid_main",
    )(x, row(pre_w), w_in.astype(_BF16), conv_w, _block_diag(pool_w).astype(_BF16),
      row(pool_scale), kbdt, vbd, w_out.astype(_BF16), row(post_w))


def kernel(x, mem, pre_norm_w, mem_norm_w, w_in, conv_w, pool_w, pool_scale, w_kv, w_out, post_norm_w):
    for l in range(pre_norm_w.shape[0]):
        x = _layer(x, mem, pre_norm_w[l], mem_norm_w[l], w_in[l], conv_w[l], pool_w[l],
                   pool_scale[l], w_kv[l], w_out[l], post_norm_w[l])
    return x
```

```python
import functools

import jax
import jax.numpy as jnp
from jax import lax
from jax.experimental import pallas as pl
from jax.experimental.pallas import tpu as pltpu

D_MODEL = 1024
N_MEM = 256
D_CONV = 384
D_POOL = 384
D_ATT = 256
N_MEM_HEADS = 4
HEAD_DIM = D_ATT // N_MEM_HEADS
CONV_WIDTH = 3
POOL_WINDOWS = (2, 4, 8, 16)
POOL_GROUP = D_POOL // len(POOL_WINDOWS)
D_IN_PROJ = 4 * D_CONV + 2 * D_POOL + 2 * D_ATT
EPS = 1e-6

_C_CONV = 0
_C_POOL = 4 * D_CONV
_C_ATT = _C_POOL + 2 * D_POOL

_LANES = 128
_SUBLANES = 8

_TILE_T = 512
_CONV_HALO = _SUBLANES
_POOL_HALO = 32
_VMEM_LIMIT_BYTES = 56 * 1024 * 1024

_BF16 = jnp.bfloat16
_F32 = jnp.float32


def _rms_scale(v):
    return lax.rsqrt(jnp.mean(v * v, axis=-1, keepdims=True) + EPS)


def _silu(v):
    return v * jax.nn.sigmoid(v)


def _kv_kernel(mem_ref, mw_ref, wkv_ref, kbdt_ref, vbd_ref):
    m = mem_ref[0]
    mn = (m * _rms_scale(m) * mw_ref[...]).astype(_BF16)
    kv = jnp.dot(mn, wkv_ref[...], preferred_element_type=_F32)
    k = kv[:, :D_ATT] * (HEAD_DIM ** -0.5)
    v = kv[:, D_ATT:]
    kt = k.T
    kt4 = jnp.concatenate([kt] * N_MEM_HEADS, axis=1)
    r = lax.broadcasted_iota(jnp.int32, kt4.shape, 0) // HEAD_DIM
    c = lax.broadcasted_iota(jnp.int32, kt4.shape, 1) // N_MEM
    kbdt_ref[0] = jnp.where(r == c, kt4, 0.0).astype(_BF16)
    v4 = jnp.concatenate([v] * N_MEM_HEADS, axis=0)
    r = lax.broadcasted_iota(jnp.int32, v4.shape, 0) // N_MEM
    c = lax.broadcasted_iota(jnp.int32, v4.shape, 1) // HEAD_DIM
    vbd_ref[0] = jnp.where(r == c, v4, 0.0).astype(_BF16)


def _main_kernel(x_ref, prew_ref, win_ref, convw_ref, poolw_ref, pscale_ref,
                 kbdt_ref, vbd_ref, wout_ref, postw_ref, o_ref,
                 ubuf, xbuf, b2, b4, b8, dbuf, ycat):
    t_rows = x_ref.shape[1]
    s = pl.program_id(1)
    ch, ph = _CONV_HALO, _POOL_HALO

    @pl.when(s == 0)
    def _():
        ubuf[0:ch, :] = jnp.zeros((ch, D_CONV), _F32)
        xbuf[0:ph, :] = jnp.zeros((ph, D_POOL), _F32)

    x = x_ref[0]
    h = (x * _rms_scale(x) * prew_ref[...]).astype(_BF16)


    pa = jnp.dot(h, win_ref[:, _C_ATT:D_IN_PROJ], preferred_element_type=_F32)
    q = pa[:, 0:D_ATT].astype(_BF16)
    ga = pa[:, D_ATT:2 * D_ATT]
    sc = jnp.dot(q, kbdt_ref[0], preferred_element_type=_F32)

    pc = jnp.dot(h, win_ref[:, _C_CONV:_C_POOL], preferred_element_type=_F32)

    probs = []
    for hd in range(N_MEM_HEADS):
        sh = sc[:, hd * N_MEM:(hd + 1) * N_MEM]
        e = jnp.exp(sh - jnp.max(sh, axis=-1, keepdims=True))
        probs.append((e * (1.0 / jnp.sum(e, axis=-1, keepdims=True))).astype(_BF16))
    p = jnp.concatenate(probs, axis=1)
    oa = jnp.dot(p, vbd_ref[0], preferred_element_type=_F32)
    ycat[:, D_CONV + D_POOL:D_MODEL] = (oa * _silu(ga)).astype(_BF16)

    pp = jnp.dot(h, win_ref[:, _C_POOL:_C_ATT], preferred_element_type=_F32)

    xc = pc[:, 0:D_CONV]
    bg = pc[:, D_CONV:2 * D_CONV]
    cg = pc[:, 2 * D_CONV:3 * D_CONV]
    gc = pc[:, 3 * D_CONV:4 * D_CONV]
    u = cg * xc
    ubuf[ch:ch + t_rows, :] = u
    conv = u * convw_ref[CONV_WIDTH - 1:CONV_WIDTH, :]
    for k in range(CONV_WIDTH - 1):
        back = CONV_WIDTH - 1 - k
        conv = conv + ubuf[ch - back:ch - back + t_rows, :] * convw_ref[k:k + 1, :]
    ubuf[0:ch, :] = ubuf[t_rows:t_rows + ch, :]
    ycat[:, 0:D_CONV] = (bg * conv * _silu(gc)).astype(_BF16)

    k0, k1 = 2 * _LANES, D_CONV + D_POOL
    y = jnp.dot(ycat[:, 0:k0], wout_ref[0:k0, :], preferred_element_type=_F32)
    y = y + jnp.dot(ycat[:, k1:D_MODEL], wout_ref[k1:D_MODEL, :], preferred_element_type=_F32)

    xp = pp[:, 0:D_POOL]
    gp = pp[:, D_POOL:2 * D_POOL]
    xbuf[ph:ph + t_rows, :] = xp
    n = t_rows + ph
    b2[8:n, :] = xbuf[8:n, :] + xbuf[7:n - 1, :]
    b4[16:n, :] = b2[16:n, :] + b2[14:n - 2, :]
    b8[24:n, 0:2 * _LANES] = b4[24:n, _LANES:] + b4[20:n - 4, _LANES:]
    s16 = b8[ph:n, _LANES:] + b8[ph - 8:n - 8, _LANES:]
    lane = lax.broadcasted_iota(jnp.int32, (1, _LANES), 1)
    w0 = jnp.where(lane < POOL_GROUP, b2[ph:n, 0:_LANES], b4[ph:n, 0:_LANES])
    w1 = jnp.where(lane < 2 * POOL_GROUP - _LANES, b4[ph:n, _LANES:2 * _LANES], b8[ph:n, 0:_LANES])
    w2 = jnp.where(lane < 3 * POOL_GROUP - 2 * _LANES, b8[ph:n, _LANES:], s16)
    wsum = jnp.concatenate([w0, w1, w2], axis=1)
    chan = lax.broadcasted_iota(jnp.int32, (1, D_POOL), 1)
    win = jnp.full((1, D_POOL), POOL_WINDOWS[-1], jnp.int32)
    for g in reversed(range(len(POOL_WINDOWS) - 1)):
        win = jnp.where(chan < (g + 1) * POOL_GROUP, POOL_WINDOWS[g], win)
    inv_win = 1.0 / win.astype(_F32)
    dbuf[...] = (wsum * inv_win - xp).astype(_BF16)

    @pl.when(s == 0)
    def _():
        top = POOL_WINDOWS[-1]
        trow = lax.broadcasted_iota(jnp.int32, (top, D_POOL), 0)
        cnt = jnp.minimum(trow + 1, win).astype(_F32)
        dbuf[0:top, :] = (wsum[0:top, :] / cnt - xp[0:top, :]).astype(_BF16)

    xbuf[0:ph, :] = xbuf[t_rows:t_rows + ph, :]
    mixed = jnp.dot(dbuf[...], poolw_ref[...], preferred_element_type=_F32)
    ycat[:, D_CONV:k1] = (mixed * pscale_ref[...] * _silu(gp)).astype(_BF16)

    y = y + jnp.dot(ycat[:, k0:k1], wout_ref[k0:k1, :], preferred_element_type=_F32)
    o_ref[0] = x_ref[0] + y * _rms_scale(y) * postw_ref[...]


def _block_diag(pw):
    g, a, b = pw.shape
    out = jnp.zeros((g * a, g * b), pw.dtype)
    for i in range(g):
        out = out.at[i * a:(i + 1) * a, i * b:(i + 1) * b].set(pw[i])
    return out


def _layer(x, mem, pre_w, mem_w, w_in, conv_w, pool_w, pool_scale, w_kv, w_out, post_w):
    b, s, d = x.shape
    m = mem.shape[1]
    assert d == D_MODEL and m == N_MEM and s % _TILE_T == 0
    hm = N_MEM_HEADS * N_MEM
    row = lambda v: v.reshape(1, -1)

    kbdt, vbd = pl.pallas_call(
        _kv_kernel,
        grid=(b,),
        in_specs=[
            pl.BlockSpec((1, m, d), lambda i: (i, 0, 0)),
            pl.BlockSpec((1, d), lambda i: (0, 0)),
            pl.BlockSpec((d, 2 * D_ATT), lambda i: (0, 0)),
        ],
        out_specs=[
            pl.BlockSpec((1, D_ATT, hm), lambda i: (i, 0, 0)),
            pl.BlockSpec((1, hm, D_ATT), lambda i: (i, 0, 0)),
        ],
        out_shape=[
            jax.ShapeDtypeStruct((b, D_ATT, hm), _BF16),
            jax.ShapeDtypeStruct((b, hm, D_ATT), _BF16),
        ],
        compiler_params=pltpu.CompilerParams(dimension_semantics=("arbitrary",)),
        name="kv_prep",
    )(mem, row(mem_w), w_kv.astype(_BF16))

    t = _TILE_T
    const = lambda shape: pl.BlockSpec(shape, lambda i, j: (0,) * len(shape))
    return pl.pallas_call(
        _main_kernel,
        grid=(b, s // t),
        in_specs=[
            pl.BlockSpec((1, t, d), lambda i, j: (i, j, 0)),
            const((1, d)),
            const((d, D_IN_PROJ)),
            const((CONV_WIDTH, D_CONV)),
            const((D_POOL, D_POOL)),
            const((1, D_POOL)),
            pl.BlockSpec((1, D_ATT, hm), lambda i, j: (i, 0, 0)),
            pl.BlockSpec((1, hm, D_ATT), lambda i, j: (i, 0, 0)),
            const((d, d)),
            const((1, d)),
        ],
        out_specs=pl.BlockSpec((1, t, d), lambda i, j: (i, j, 0)),
        out_shape=jax.ShapeDtypeStruct((b, s, d), x.dtype),
        scratch_shapes=[
            pltpu.VMEM((_CONV_HALO + t, D_CONV), _F32),
            pltpu.VMEM((_POOL_HALO + t, D_POOL), _F32),
            pltpu.VMEM((_POOL_HALO + t, D_POOL), _F32),
            pltpu.VMEM((_POOL_HALO + t, D_POOL), _F32),
            pltpu.VMEM((_POOL_HALO + t, 2 * _LANES), _F32),
            pltpu.VMEM((t, D_POOL), _BF16),
            pltpu.VMEM((t, D_MODEL), _BF16),
        ],
        compiler_params=pltpu.CompilerParams(
            dimension_semantics=("arbitrary", "arbitrary"),
            vmem_limit_bytes=_VMEM_LIMIT_BYTES,
        ),
        name="hybrid_main",
    )(x, row(pre_w), w_in.astype(_BF16), conv_w, _block_diag(pool_w).astype(_BF16),
      row(pool_scale), kbdt, vbd, w_out.astype(_BF16), row(post_w))


def kernel(x, mem, pre_norm_w, mem_norm_w, w_in, conv_w, pool_w, pool_scale, w_kv, w_out, post_norm_w):
    for l in range(pre_norm_w.shape[0]):
        x = _layer(x, mem, pre_norm_w[l], mem_norm_w[l], w_in[l], conv_w[l], pool_w[l],
                   pool_scale[l], w_kv[l], w_out[l], post_norm_w[l])
    return x
```

```python
import functools

import jax
import jax.numpy as jnp
from jax import lax
from jax.experimental import pallas as pl
from jax.experimental.pallas import tpu as pltpu

D_MODEL = 1024
N_MEM = 256
D_CONV = 384
D_POOL = 384
D_ATT = 256
N_MEM_HEADS = 4
HEAD_DIM = D_ATT // N_MEM_HEADS
CONV_WIDTH = 3
POOL_WINDOWS = (2, 4, 8, 16)
POOL_GROUP = D_POOL // len(POOL_WINDOWS)
D_IN_PROJ = 4 * D_CONV + 2 * D_POOL + 2 * D_ATT
EPS = 1e-6

_C_CONV = 0
_C_POOL = 4 * D_CONV
_C_ATT = _C_POOL + 2 * D_POOL

_LANES = 128
_SUBLANES = 8

_TILE_T = 1024
_CONV_HALO = _SUBLANES
_POOL_HALO = 32
_VMEM_LIMIT_BYTES = 56 * 1024 * 1024

_BF16 = jnp.bfloat16
_F32 = jnp.float32


def _rms_scale(v):
    return lax.rsqrt(jnp.mean(v * v, axis=-1, keepdims=True) + EPS)


def _silu(v):
    return v * jax.nn.sigmoid(v)


def _kv_kernel(mem_ref, mw_ref, wkv_ref, kbdt_ref, vbd_ref):
    m = mem_ref[0]
    mn = (m * _rms_scale(m) * mw_ref[...]).astype(_BF16)
    kv = jnp.dot(mn, wkv_ref[...], preferred_element_type=_F32)
    k = kv[:, :D_ATT] * (HEAD_DIM ** -0.5)
    v = kv[:, D_ATT:]
    kt = k.T
    kt4 = jnp.concatenate([kt] * N_MEM_HEADS, axis=1)
    r = lax.broadcasted_iota(jnp.int32, kt4.shape, 0) // HEAD_DIM
    c = lax.broadcasted_iota(jnp.int32, kt4.shape, 1) // N_MEM
    kbdt_ref[0] = jnp.where(r == c, kt4, 0.0).astype(_BF16)
    v4 = jnp.concatenate([v] * N_MEM_HEADS, axis=0)
    r = lax.broadcasted_iota(jnp.int32, v4.shape, 0) // N_MEM
    c = lax.broadcasted_iota(jnp.int32, v4.shape, 1) // HEAD_DIM
    vbd_ref[0] = jnp.where(r == c, v4, 0.0).astype(_BF16)


def _main_kernel(x_ref, prew_ref, win_ref, convw_ref, poolw_ref, pscale_ref,
                 kbdt_ref, vbd_ref, wout_ref, postw_ref, o_ref,
                 ubuf, xbuf, b2, b4, b8, dbuf, ycat):
    t_rows = x_ref.shape[1]
    s = pl.program_id(1)
    ch, ph = _CONV_HALO, _POOL_HALO

    @pl.when(s == 0)
    def _():
        ubuf[0:ch, :] = jnp.zeros((ch, D_CONV), _F32)
        xbuf[0:ph, :] = jnp.zeros((ph, D_POOL), _F32)

    x = x_ref[0]
    h = (x * _rms_scale(x) * prew_ref[...]).astype(_BF16)


    pa = jnp.dot(h, win_ref[:, _C_ATT:D_IN_PROJ], preferred_element_type=_F32)
    q = pa[:, 0:D_ATT].astype(_BF16)
    ga = pa[:, D_ATT:2 * D_ATT]
    sc = jnp.dot(q, kbdt_ref[0], preferred_element_type=_F32)

    pc = jnp.dot(h, win_ref[:, _C_CONV:_C_POOL], preferred_element_type=_F32)

    probs = []
    for hd in range(N_MEM_HEADS):
        sh = sc[:, hd * N_MEM:(hd + 1) * N_MEM]
        e = jnp.exp(sh - jnp.max(sh, axis=-1, keepdims=True))
        probs.append((e * (1.0 / jnp.sum(e, axis=-1, keepdims=True))).astype(_BF16))
    p = jnp.concatenate(probs, axis=1)
    oa = jnp.dot(p, vbd_ref[0], preferred_element_type=_F32)
    ycat[:, D_CONV + D_POOL:D_MODEL] = (oa * _silu(ga)).astype(_BF16)

    pp = jnp.dot(h, win_ref[:, _C_POOL:_C_ATT], preferred_element_type=_F32)

    xc = pc[:, 0:D_CONV]
    bg = pc[:, D_CONV:2 * D_CONV]
    cg = pc[:, 2 * D_CONV:3 * D_CONV]
    gc = pc[:, 3 * D_CONV:4 * D_CONV]
    u = cg * xc
    ubuf[ch:ch + t_rows, :] = u
    conv = u * convw_ref[CONV_WIDTH - 1:CONV_WIDTH, :]
    for k in range(CONV_WIDTH - 1):
        back = CONV_WIDTH - 1 - k
        conv = conv + ubuf[ch - back:ch - back + t_rows, :] * convw_ref[k:k + 1, :]
    ubuf[0:ch, :] = ubuf[t_rows:t_rows + ch, :]
    ycat[:, 0:D_CONV] = (bg * conv * _silu(gc)).astype(_BF16)

    k0, k1 = 2 * _LANES, D_CONV + D_POOL
    y = jnp.dot(ycat[:, 0:k0], wout_ref[0:k0, :], preferred_element_type=_F32)
    y = y + jnp.dot(ycat[:, k1:D_MODEL], wout_ref[k1:D_MODEL, :], preferred_element_type=_F32)

    xp = pp[:, 0:D_POOL]
    gp = pp[:, D_POOL:2 * D_POOL]
    xbuf[ph:ph + t_rows, :] = xp
    n = t_rows + ph
    b2[8:n, :] = xbuf[8:n, :] + xbuf[7:n - 1, :]
    b4[16:n, :] = b2[16:n, :] + b2[14:n - 2, :]
    b8[24:n, 0:2 * _LANES] = b4[24:n, _LANES:] + b4[20:n - 4, _LANES:]
    s16 = b8[ph:n, _LANES:] + b8[ph - 8:n - 8, _LANES:]
    lane = lax.broadcasted_iota(jnp.int32, (1, _LANES), 1)
    w0 = jnp.where(lane < POOL_GROUP, b2[ph:n, 0:_LANES], b4[ph:n, 0:_LANES])
    w1 = jnp.where(lane < 2 * POOL_GROUP - _LANES, b4[ph:n, _LANES:2 * _LANES], b8[ph:n, 0:_LANES])
    w2 = jnp.where(lane < 3 * POOL_GROUP - 2 * _LANES, b8[ph:n, _LANES:], s16)
    wsum = jnp.concatenate([w0, w1, w2], axis=1)
    chan = lax.broadcasted_iota(jnp.int32, (1, D_POOL), 1)
    win = jnp.full((1, D_POOL), POOL_WINDOWS[-1], jnp.int32)
    for g in reversed(range(len(POOL_WINDOWS) - 1)):
        win = jnp.where(chan < (g + 1) * POOL_GROUP, POOL_WINDOWS[g], win)
    inv_win = 1.0 / win.astype(_F32)
    dbuf[...] = (wsum * inv_win - xp).astype(_BF16)

    @pl.when(s == 0)
    def _():
        top = POOL_WINDOWS[-1]
        trow = lax.broadcasted_iota(jnp.int32, (top, D_POOL), 0)
        cnt = jnp.minimum(trow + 1, win).astype(_F32)
        dbuf[0:top, :] = (wsum[0:top, :] / cnt - xp[0:top, :]).astype(_BF16)

    xbuf[0:ph, :] = xbuf[t_rows:t_rows + ph, :]
    mixed = jnp.dot(dbuf[...], poolw_ref[...], preferred_element_type=_F32)
    ycat[:, D_CONV:k1] = (mixed * pscale_ref[...] * _silu(gp)).astype(_BF16)

    y = y + jnp.dot(ycat[:, k0:k1], wout_ref[k0:k1, :], preferred_element_type=_F32)
    o_ref[0] = x_ref[0] + y * _rms_scale(y) * postw_ref[...]


def _block_diag(pw):
    g, a, b = pw.shape
    out = jnp.zeros((g * a, g * b), pw.dtype)
    for i in range(g):
        out = out.at[i * a:(i + 1) * a, i * b:(i + 1) * b].set(pw[i])
    return out


def _layer(x, mem, pre_w, mem_w, w_in, conv_w, pool_w, pool_scale, w_kv, w_out, post_w):
    b, s, d = x.shape
    m = mem.shape[1]
    assert d == D_MODEL and m == N_MEM and s % _TILE_T == 0
    hm = N_MEM_HEADS * N_MEM
    row = lambda v: v.reshape(1, -1)

    kbdt, vbd = pl.pallas_call(
        _kv_kernel,
        grid=(b,),
        in_specs=[
            pl.BlockSpec((1, m, d), lambda i: (i, 0, 0)),
            pl.BlockSpec((1, d), lambda i: (0, 0)),
            pl.BlockSpec((d, 2 * D_ATT), lambda i: (0, 0)),
        ],
        out_specs=[
            pl.BlockSpec((1, D_ATT, hm), lambda i: (i, 0, 0)),
            pl.BlockSpec((1, hm, D_ATT), lambda i: (i, 0, 0)),
        ],
        out_shape=[
            jax.ShapeDtypeStruct((b, D_ATT, hm), _BF16),
            jax.ShapeDtypeStruct((b, hm, D_ATT), _BF16),
        ],
        compiler_params=pltpu.CompilerParams(dimension_semantics=("arbitrary",)),
        name="kv_prep",
    )(mem, row(mem_w), w_kv.astype(_BF16))

    t = _TILE_T
    const = lambda shape: pl.BlockSpec(shape, lambda i, j: (0,) * len(shape))
    return pl.pallas_call(
        _main_kernel,
        grid=(b, s // t),
        in_specs=[
            pl.BlockSpec((1, t, d), lambda i, j: (i, j, 0)),
            const((1, d)),
            const((d, D_IN_PROJ)),
            const((CONV_WIDTH, D_CONV)),
            const((D_POOL, D_POOL)),
            const((1, D_POOL)),
            pl.BlockSpec((1, D_ATT, hm), lambda i, j: (i, 0, 0)),
            pl.BlockSpec((1, hm, D_ATT), lambda i, j: (i, 0, 0)),
            const((d, d)),
            const((1, d)),
        ],
        out_specs=pl.BlockSpec((1, t, d), lambda i, j: (i, j, 0)),
        out_shape=jax.ShapeDtypeStruct((b, s, d), x.dtype),
        scratch_shapes=[
            pltpu.VMEM((_CONV_HALO + t, D_CONV), _F32),
            pltpu.VMEM((_POOL_HALO + t, D_POOL), _F32),
            pltpu.VMEM((_POOL_HALO + t, D_POOL), _F32),
            pltpu.VMEM((_POOL_HALO + t, D_POOL), _F32),
            pltpu.VMEM((_POOL_HALO + t, 2 * _LANES), _F32),
            pltpu.VMEM((t, D_POOL), _BF16),
            pltpu.VMEM((t, D_MODEL), _BF16),
        ],
        compiler_params=pltpu.CompilerParams(
            dimension_semantics=("arbitrary", "arbitrary"),
            vmem_limit_bytes=_VMEM_LIMIT_BYTES,
        ),
        name="hybrid_main",
    )(x, row(pre_w), w_in.astype(_BF16), conv_w, _block_diag(pool_w).astype(_BF16),
      row(pool_scale), kbdt, vbd, w_out.astype(_BF16), row(post_w))


def kernel(x, mem, pre_norm_w, mem_norm_w, w_in, conv_w, pool_w, pool_scale, w_kv, w_out, post_norm_w):
    for l in range(pre_norm_w.shape[0]):
        x = _layer(x, mem, pre_norm_w[l], mem_norm_w[l], w_in[l], conv_w[l], pool_w[l],
                   pool_scale[l], w_kv[l], w_out[l], post_norm_w[l])
    return x
```

```python
import jax
import jax.numpy as jnp
from jax import lax
from jax.experimental import pallas as pl
from jax.experimental.pallas import tpu as pltpu

D_MODEL = 1024
N_MEM = 256
D_CONV = 384
D_POOL = 384
D_ATT = 256
N_MEM_HEADS = 4
HEAD_DIM = D_ATT // N_MEM_HEADS
CONV_WIDTH = 3
POOL_WINDOWS = (2, 4, 8, 16)
POOL_GROUP = D_POOL // len(POOL_WINDOWS)
D_IN_PROJ = 4 * D_CONV + 2 * D_POOL + 2 * D_ATT
EPS = 1e-6

_C_CONV = 0
_C_POOL = 4 * D_CONV
_C_ATT = _C_POOL + 2 * D_POOL

_LANES = 128
_SUBLANES = 8

_TILE_T = 1024
_CHUNK_T = 512
_CONV_HALO = _SUBLANES
_POOL_HALO = 32
_VMEM_LIMIT_BYTES = 56 * 1024 * 1024

_BF16 = jnp.bfloat16
_F32 = jnp.float32


def _rms_scale(v):
    return lax.rsqrt(jnp.mean(v * v, axis=-1, keepdims=True) + EPS)


def _silu(v):
    return v * jax.nn.sigmoid(v)


def _kv_kernel(mem_ref, mw_ref, wkv_ref, kbdt_ref, vbd_ref):
    m = mem_ref[0]
    mn = (m * _rms_scale(m) * mw_ref[...]).astype(_BF16)
    kv = jnp.dot(mn, wkv_ref[...], preferred_element_type=_F32)
    k = kv[:, :D_ATT] * (HEAD_DIM ** -0.5)
    v = kv[:, D_ATT:]
    kt = k.T
    kt4 = jnp.concatenate([kt] * N_MEM_HEADS, axis=1)
    r = lax.broadcasted_iota(jnp.int32, kt4.shape, 0) // HEAD_DIM
    c = lax.broadcasted_iota(jnp.int32, kt4.shape, 1) // N_MEM
    kbdt_ref[0] = jnp.where(r == c, kt4, 0.0).astype(_BF16)
    v4 = jnp.concatenate([v] * N_MEM_HEADS, axis=0)
    r = lax.broadcasted_iota(jnp.int32, v4.shape, 0) // N_MEM
    c = lax.broadcasted_iota(jnp.int32, v4.shape, 1) // HEAD_DIM
    vbd_ref[0] = jnp.where(r == c, v4, 0.0).astype(_BF16)


def _main_kernel(x_ref, prew_ref, win_ref, convw_ref, poolw_ref, pscale_ref,
                 kbdt_ref, vbd_ref, wout_ref, postw_ref, o_ref,
                 ubuf, xbuf, b2, b4, b8, dbuf, ycat, hbuf):
    t_rows = x_ref.shape[1]
    s = pl.program_id(1)
    ch, ph = _CONV_HALO, _POOL_HALO

    @pl.when(s == 0)
    def _():
        ubuf[0:ch, :] = jnp.zeros((ch, D_CONV), _F32)
        xbuf[0:ph, :] = jnp.zeros((ph, D_POOL), _F32)

    lane = lax.broadcasted_iota(jnp.int32, (1, _LANES), 1)
    chan = lax.broadcasted_iota(jnp.int32, (1, D_POOL), 1)
    win = jnp.full((1, D_POOL), POOL_WINDOWS[-1], jnp.int32)
    for g in reversed(range(len(POOL_WINDOWS) - 1)):
        win = jnp.where(chan < (g + 1) * POOL_GROUP, POOL_WINDOWS[g], win)
    inv_win = 1.0 / win.astype(_F32)
    k0, k1 = 2 * _LANES, D_CONV + D_POOL

    class Chunk:
        def __init__(self, r0, rows):
            self.r0, self.rows, self.rs = r0, rows, slice(r0, r0 + rows)

        def prenorm(self):
            x = x_ref[0, self.rs, :]
            hbuf[self.rs, :] = (x * _rms_scale(x) * prew_ref[...]).astype(_BF16)

        def proj_att(self):
            self.pa = jnp.dot(hbuf[self.rs, :], win_ref[:, _C_ATT:D_IN_PROJ], preferred_element_type=_F32)

        def proj_pool(self):
            self.pp = jnp.dot(hbuf[self.rs, :], win_ref[:, _C_POOL:_C_ATT], preferred_element_type=_F32)

        def proj_conv(self):
            self.pc = jnp.dot(hbuf[self.rs, :], win_ref[:, _C_CONV:_C_POOL], preferred_element_type=_F32)

        def scores(self):
            q = self.pa[:, 0:D_ATT].astype(_BF16)
            self.sc = jnp.dot(q, kbdt_ref[0], preferred_element_type=_F32)

        def softmax_values(self):
            probs = []
            for hd in range(N_MEM_HEADS):
                sh = self.sc[:, hd * N_MEM:(hd + 1) * N_MEM]
                e = jnp.exp(sh - jnp.max(sh, axis=-1, keepdims=True))
                probs.append((e * (1.0 / jnp.sum(e, axis=-1, keepdims=True))).astype(_BF16))
            p = jnp.concatenate(probs, axis=1)
            self.oa = jnp.dot(p, vbd_ref[0], preferred_element_type=_F32)

        def pool_mix(self):
            r0, rows, rs = self.r0, self.rows, self.rs
            xp = self.pp[:, 0:D_POOL]
            lo, hi = ph + r0, ph + r0 + rows
            xbuf[lo:hi, :] = xp
            l2, l4, l8 = (8, 16, 24) if r0 == 0 else (lo, lo, lo)
            b2[l2:hi, :] = xbuf[l2:hi, :] + xbuf[l2 - 1:hi - 1, :]
            b4[l4:hi, :] = b2[l4:hi, :] + b2[l4 - 2:hi - 2, :]
            b8[l8:hi, 0:2 * _LANES] = b4[l8:hi, _LANES:] + b4[l8 - 4:hi - 4, _LANES:]
            s16 = b8[lo:hi, _LANES:] + b8[lo - 8:hi - 8, _LANES:]
            w0 = jnp.where(lane < POOL_GROUP, b2[lo:hi, 0:_LANES], b4[lo:hi, 0:_LANES])
            w1 = jnp.where(lane < 2 * POOL_GROUP - _LANES, b4[lo:hi, _LANES:2 * _LANES], b8[lo:hi, 0:_LANES])
            w2 = jnp.where(lane < 3 * POOL_GROUP - 2 * _LANES, b8[lo:hi, _LANES:], s16)
            wsum = jnp.concatenate([w0, w1, w2], axis=1)
            dbuf[rs, :] = (wsum * inv_win - xp).astype(_BF16)
            if r0 == 0:
                top = POOL_WINDOWS[-1]
                trow = lax.broadcasted_iota(jnp.int32, (top, D_POOL), 0) + s * t_rows
                cnt = jnp.minimum(trow + 1, win).astype(_F32)
                dbuf[0:top, :] = (wsum[0:top, :] / cnt - xp[0:top, :]).astype(_BF16)
            self.mixed = jnp.dot(dbuf[rs, :], poolw_ref[...], preferred_element_type=_F32)

        def out_att(self):
            rs = self.rs
            ga = self.pa[:, D_ATT:2 * D_ATT]
            ycat[rs, k1:D_MODEL] = (self.oa * _silu(ga)).astype(_BF16)
            self.y = jnp.dot(ycat[rs, k1:D_MODEL], wout_ref[k1:D_MODEL, :], preferred_element_type=_F32)

        def out_conv(self):
            r0, rows, rs, pc = self.r0, self.rows, self.rs, self.pc
            xc = pc[:, 0:D_CONV]
            bg = pc[:, D_CONV:2 * D_CONV]
            cg = pc[:, 2 * D_CONV:3 * D_CONV]
            gc = pc[:, 3 * D_CONV:4 * D_CONV]
            u = cg * xc
            ubuf[ch + r0:ch + r0 + rows, :] = u
            conv = u * convw_ref[CONV_WIDTH - 1:CONV_WIDTH, :]
            for k in range(CONV_WIDTH - 1):
                back = CONV_WIDTH - 1 - k
                conv = conv + ubuf[ch + r0 - back:ch + r0 - back + rows, :] * convw_ref[k:k + 1, :]
            ycat[rs, 0:D_CONV] = (bg * conv * _silu(gc)).astype(_BF16)
            self.y = self.y + jnp.dot(ycat[rs, 0:k0], wout_ref[0:k0, :], preferred_element_type=_F32)

        def out_pool_postnorm(self):
            rs = self.rs
            gp = self.pp[:, D_POOL:2 * D_POOL]
            ycat[rs, D_CONV:k1] = (self.mixed * pscale_ref[...] * _silu(gp)).astype(_BF16)
            y = self.y + jnp.dot(ycat[rs, k0:k1], wout_ref[k0:k1, :], preferred_element_type=_F32)
            o_ref[0, rs, :] = x_ref[0, rs, :] + y * _rms_scale(y) * postw_ref[...]

    chunks = [Chunk(c * _CHUNK_T, _CHUNK_T) for c in range(t_rows // _CHUNK_T)]
    for c in chunks:
        c.prenorm(); c.proj_att(); c.proj_pool(); c.scores(); c.proj_conv()
        c.softmax_values(); c.pool_mix(); c.out_att(); c.out_conv(); c.out_pool_postnorm()

    ubuf[0:ch, :] = ubuf[t_rows:t_rows + ch, :]
    xbuf[0:ph, :] = xbuf[t_rows:t_rows + ph, :]


def _block_diag(pw):
    g, a, b = pw.shape
    out = jnp.zeros((g * a, g * b), pw.dtype)
    for i in range(g):
        out = out.at[i * a:(i + 1) * a, i * b:(i + 1) * b].set(pw[i])
    return out


def _layer(x, mem, pre_w, mem_w, w_in, conv_w, pool_w, pool_scale, w_kv, w_out, post_w):
    b, s, d = x.shape
    m = mem.shape[1]
    assert d == D_MODEL and m == N_MEM and s % _TILE_T == 0 and _TILE_T % _CHUNK_T == 0
    hm = N_MEM_HEADS * N_MEM
    row = lambda v: v.reshape(1, -1)

    kbdt, vbd = pl.pallas_call(
        _kv_kernel,
        grid=(b,),
        in_specs=[
            pl.BlockSpec((1, m, d), lambda i: (i, 0, 0)),
            pl.BlockSpec((1, d), lambda i: (0, 0)),
            pl.BlockSpec((d, 2 * D_ATT), lambda i: (0, 0)),
        ],
        out_specs=[
            pl.BlockSpec((1, D_ATT, hm), lambda i: (i, 0, 0)),
            pl.BlockSpec((1, hm, D_ATT), lambda i: (i, 0, 0)),
        ],
        out_shape=[
            jax.ShapeDtypeStruct((b, D_ATT, hm), _BF16),
            jax.ShapeDtypeStruct((b, hm, D_ATT), _BF16),
        ],
        compiler_params=pltpu.CompilerParams(dimension_semantics=("arbitrary",)),
        name="kv_prep",
    )(mem, row(mem_w), w_kv.astype(_BF16))

    t = _TILE_T
    const = lambda shape: pl.BlockSpec(shape, lambda i, j: (0,) * len(shape))
    return pl.pallas_call(
        _main_kernel,
        grid=(b, s // t),
        in_specs=[
            pl.BlockSpec((1, t, d), lambda i, j: (i, j, 0)),
            const((1, d)),
            const((d, D_IN_PROJ)),
            const((CONV_WIDTH, D_CONV)),
            const((D_POOL, D_POOL)),
            const((1, D_POOL)),
            pl.BlockSpec((1, D_ATT, hm), lambda i, j: (i, 0, 0)),
            pl.BlockSpec((1, hm, D_ATT), lambda i, j: (i, 0, 0)),
            const((d, d)),
            const((1, d)),
        ],
        out_specs=pl.BlockSpec((1, t, d), lambda i, j: (i, j, 0)),
        out_shape=jax.ShapeDtypeStruct((b, s, d), x.dtype),
        scratch_shapes=[
            pltpu.VMEM((_CONV_HALO + t, D_CONV), _F32),
            pltpu.VMEM((_POOL_HALO + t, D_POOL), _F32),
            pltpu.VMEM((_POOL_HALO + t, D_POOL), _F32),
            pltpu.VMEM((_POOL_HALO + t, D_POOL), _F32),
            pltpu.VMEM((_POOL_HALO + t, 2 * _LANES), _F32),
            pltpu.VMEM((t, D_POOL), _BF16),
            pltpu.VMEM((t, D_MODEL), _BF16),
            pltpu.VMEM((t, D_MODEL), _BF16),
        ],
        compiler_params=pltpu.CompilerParams(
            dimension_semantics=("arbitrary", "arbitrary"),
            vmem_limit_bytes=_VMEM_LIMIT_BYTES,
        ),
        name="hybrid_main",
    )(x, row(pre_w), w_in.astype(_BF16), conv_w, _block_diag(pool_w).astype(_BF16),
      row(pool_scale), kbdt, vbd, w_out.astype(_BF16), row(post_w))


def kernel(x, mem, pre_norm_w, mem_norm_w, w_in, conv_w, pool_w, pool_scale, w_kv, w_out, post_norm_w):
    for l in range(pre_norm_w.shape[0]):
        x = _layer(x, mem, pre_norm_w[l], mem_norm_w[l], w_in[l], conv_w[l], pool_w[l],
                   pool_scale[l], w_kv[l], w_out[l], post_norm_w[l])
    return x
```

```python
import jax
import jax.numpy as jnp
from jax import lax
from jax.experimental import pallas as pl
from jax.experimental.pallas import tpu as pltpu

D_MODEL = 1024
N_MEM = 256
D_CONV = 384
D_POOL = 384
D_ATT = 256
N_MEM_HEADS = 4
HEAD_DIM = D_ATT // N_MEM_HEADS
CONV_WIDTH = 3
POOL_WINDOWS = (2, 4, 8, 16)
POOL_GROUP = D_POOL // len(POOL_WINDOWS)
D_IN_PROJ = 4 * D_CONV + 2 * D_POOL + 2 * D_ATT
EPS = 1e-6

_C_CONV = 0
_C_POOL = 4 * D_CONV
_C_ATT = _C_POOL + 2 * D_POOL

_LANES = 128
_SUBLANES = 8

_TILE_T = 1024
_CHUNK_T = 512
_CONV_HALO = _SUBLANES
_POOL_HALO = 32
_STAGE_ROWS = 128
_VMEM_LIMIT_BYTES = 56 * 1024 * 1024

_BF16 = jnp.bfloat16
_F32 = jnp.float32


def _rms_scale(v):
    return lax.rsqrt(jnp.mean(v * v, axis=-1, keepdims=True) + EPS)


def _silu(v):
    return v * jax.nn.sigmoid(v)


def _memory_kv(mem_ref, mw_ref, wkv_bf, kbdt, vbd):
    m = mem_ref[0]
    mn = (m * _rms_scale(m) * mw_ref[...]).astype(_BF16)
    kv = jnp.dot(mn, wkv_bf[...], preferred_element_type=_F32)
    k = kv[:, :D_ATT] * (HEAD_DIM ** -0.5)
    v = kv[:, D_ATT:]
    kt = k.T
    kt4 = jnp.concatenate([kt] * N_MEM_HEADS, axis=1)
    r = lax.broadcasted_iota(jnp.int32, kt4.shape, 0) // HEAD_DIM
    c = lax.broadcasted_iota(jnp.int32, kt4.shape, 1) // N_MEM
    kbdt[...] = jnp.where(r == c, kt4, 0.0).astype(_BF16)
    v4 = jnp.concatenate([v] * N_MEM_HEADS, axis=0)
    r = lax.broadcasted_iota(jnp.int32, v4.shape, 0) // N_MEM
    c = lax.broadcasted_iota(jnp.int32, v4.shape, 1) // HEAD_DIM
    vbd[...] = jnp.where(r == c, v4, 0.0).astype(_BF16)


def _load_weights_bf16(jobs, stage, sem):
    rows = stage.shape[1]
    chunks = [(src, dst, r0) for src, dst in jobs for r0 in range(0, src.shape[0], rows)]

    def copy(i):
        src, _, r0 = chunks[i]
        return pltpu.make_async_copy(src.at[pl.ds(r0, rows), :],
                                     stage.at[i % 2, :, pl.ds(0, src.shape[1])], sem.at[i % 2])

    copy(0).start()
    for i, (src, dst, r0) in enumerate(chunks):
        if i + 1 < len(chunks):
            copy(i + 1).start()
        copy(i).wait()
        dst[r0:r0 + rows, :] = stage[i % 2, :, 0:src.shape[1]].astype(_BF16)


def _main_kernel(x_ref, mem_ref, prew_ref, memw_ref, win_hbm, convw_ref, poolw_ref, pscale_ref,
                 wkv_hbm, wout_hbm, postw_ref, o_ref,
                 win_ref, wout_ref, wkv_bf, stage, sem, kbdt, vbd, ubuf, xbuf, b2, b4, b8, dbuf, ycat):
    t_rows = x_ref.shape[1]
    s = pl.program_id(1)
    ch, ph = _CONV_HALO, _POOL_HALO

    @pl.when((pl.program_id(0) == 0) & (s == 0))
    def _():
        _load_weights_bf16([(win_hbm, win_ref), (wkv_hbm, wkv_bf), (wout_hbm, wout_ref)], stage, sem)

    @pl.when(s == 0)
    def _():
        _memory_kv(mem_ref, memw_ref, wkv_bf, kbdt, vbd)
        ubuf[0:ch, :] = jnp.zeros((ch, D_CONV), _F32)
        xbuf[0:ph, :] = jnp.zeros((ph, D_POOL), _F32)

    lane = lax.broadcasted_iota(jnp.int32, (1, _LANES), 1)
    chan = lax.broadcasted_iota(jnp.int32, (1, D_POOL), 1)
    win = jnp.full((1, D_POOL), POOL_WINDOWS[-1], jnp.int32)
    for g in reversed(range(len(POOL_WINDOWS) - 1)):
        win = jnp.where(chan < (g + 1) * POOL_GROUP, POOL_WINDOWS[g], win)
    inv_win = 1.0 / win.astype(_F32)
    k0, k1 = 2 * _LANES, D_CONV + D_POOL

    class Chunk:
        def __init__(self, r0, rows):
            self.r0, self.rows, self.rs = r0, rows, slice(r0, r0 + rows)

        def prenorm(self):
            x = x_ref[0, self.rs, :]
            self.h = (x * _rms_scale(x) * prew_ref[...]).astype(_BF16)

        def proj_att(self):
            self.pa = jnp.dot(self.h, win_ref[:, _C_ATT:D_IN_PROJ], preferred_element_type=_F32)

        def proj_pool(self):
            self.pp = jnp.dot(self.h, win_ref[:, _C_POOL:_C_ATT], preferred_element_type=_F32)

        def proj_conv(self):
            self.pc = jnp.dot(self.h, win_ref[:, _C_CONV:_C_POOL], preferred_element_type=_F32)

        def scores(self):
            q = self.pa[:, 0:D_ATT].astype(_BF16)
            self.sc = jnp.dot(q, kbdt[...], preferred_element_type=_F32)

        def softmax_values(self):
            probs = []
            for hd in range(N_MEM_HEADS):
                sh = self.sc[:, hd * N_MEM:(hd + 1) * N_MEM]
                e = jnp.exp(sh - jnp.max(sh, axis=-1, keepdims=True))
                probs.append((e * (1.0 / jnp.sum(e, axis=-1, keepdims=True))).astype(_BF16))
            p = jnp.concatenate(probs, axis=1)
            self.oa = jnp.dot(p, vbd[...], preferred_element_type=_F32)

        def pool_mix(self):
            r0, rows, rs = self.r0, self.rows, self.rs
            xp = self.pp[:, 0:D_POOL]
            lo, hi = ph + r0, ph + r0 + rows
            xbuf[lo:hi, :] = xp
            l2, l4, l8 = (8, 16, 24) if r0 == 0 else (lo, lo, lo)
            b2[l2:hi, :] = xbuf[l2:hi, :] + xbuf[l2 - 1:hi - 1, :]
            b4[l4:hi, :] = b2[l4:hi, :] + b2[l4 - 2:hi - 2, :]
            b8[l8:hi, 0:2 * _LANES] = b4[l8:hi, _LANES:] + b4[l8 - 4:hi - 4, _LANES:]
            s16 = b8[lo:hi, _LANES:] + b8[lo - 8:hi - 8, _LANES:]
            w0 = jnp.where(lane < POOL_GROUP, b2[lo:hi, 0:_LANES], b4[lo:hi, 0:_LANES])
            w1 = jnp.where(lane < 2 * POOL_GROUP - _LANES, b4[lo:hi, _LANES:2 * _LANES], b8[lo:hi, 0:_LANES])
            w2 = jnp.where(lane < 3 * POOL_GROUP - 2 * _LANES, b8[lo:hi, _LANES:], s16)
            wsum = jnp.concatenate([w0, w1, w2], axis=1)
            dbuf[rs, :] = (wsum * inv_win - xp).astype(_BF16)
            if r0 == 0:
                top = POOL_WINDOWS[-1]
                trow = lax.broadcasted_iota(jnp.int32, (top, D_POOL), 0) + s * t_rows
                cnt = jnp.minimum(trow + 1, win).astype(_F32)
                dbuf[0:top, :] = (wsum[0:top, :] / cnt - xp[0:top, :]).astype(_BF16)
            self.mixed = jnp.dot(dbuf[rs, :], poolw_ref[...], preferred_element_type=_F32)

        def out_att(self):
            rs = self.rs
            ga = self.pa[:, D_ATT:2 * D_ATT]
            ycat[rs, k1:D_MODEL] = (self.oa * _silu(ga)).astype(_BF16)
            self.y = jnp.dot(ycat[rs, k1:D_MODEL], wout_ref[k1:D_MODEL, :], preferred_element_type=_F32)

        def out_conv(self):
            r0, rows, rs, pc = self.r0, self.rows, self.rs, self.pc
            xc = pc[:, 0:D_CONV]
            bg = pc[:, D_CONV:2 * D_CONV]
            cg = pc[:, 2 * D_CONV:3 * D_CONV]
            gc = pc[:, 3 * D_CONV:4 * D_CONV]
            u = cg * xc
            ubuf[ch + r0:ch + r0 + rows, :] = u
            conv = u * convw_ref[CONV_WIDTH - 1:CONV_WIDTH, :]
            for k in range(CONV_WIDTH - 1):
                back = CONV_WIDTH - 1 - k
                conv = conv + ubuf[ch + r0 - back:ch + r0 - back + rows, :] * convw_ref[k:k + 1, :]
            ycat[rs, 0:D_CONV] = (bg * conv * _silu(gc)).astype(_BF16)
            self.y = self.y + jnp.dot(ycat[rs, 0:k0], wout_ref[0:k0, :], preferred_element_type=_F32)

        def out_pool_postnorm(self):
            rs = self.rs
            gp = self.pp[:, D_POOL:2 * D_POOL]
            ycat[rs, D_CONV:k1] = (self.mixed * pscale_ref[...] * _silu(gp)).astype(_BF16)
            y = self.y + jnp.dot(ycat[rs, k0:k1], wout_ref[k0:k1, :], preferred_element_type=_F32)
            o_ref[0, rs, :] = x_ref[0, rs, :] + y * _rms_scale(y) * postw_ref[...]

    chunks = [Chunk(c * _CHUNK_T, _CHUNK_T) for c in range(t_rows // _CHUNK_T)]
    for c in chunks:
        c.prenorm(); c.proj_att(); c.proj_pool(); c.scores(); c.proj_conv()
        c.softmax_values(); c.pool_mix(); c.out_att(); c.out_conv(); c.out_pool_postnorm()

    ubuf[0:ch, :] = ubuf[t_rows:t_rows + ch, :]
    xbuf[0:ph, :] = xbuf[t_rows:t_rows + ph, :]


def _block_diag(pw):
    g, a, b = pw.shape
    same_group = jnp.eye(g, dtype=bool)[:, None, :, None]
    return jnp.where(same_group, pw[:, :, None, :], 0).reshape(g * a, g * b)


def _layer(x, mem, pre_w, mem_w, w_in, conv_w, pool_w, pool_scale, w_kv, w_out, post_w):
    b, s, d = x.shape
    m = mem.shape[1]
    assert d == D_MODEL and m == N_MEM and s % _TILE_T == 0 and _TILE_T % _CHUNK_T == 0 and d % _STAGE_ROWS == 0
    hm = N_MEM_HEADS * N_MEM
    row = lambda v: v.reshape(1, -1)
    t = _TILE_T
    const = lambda shape: pl.BlockSpec(shape, lambda i, j: (0,) * len(shape))
    in_hbm = pl.BlockSpec(memory_space=pl.ANY)
    return pl.pallas_call(
        _main_kernel,
        grid=(b, s // t),
        in_specs=[
            pl.BlockSpec((1, t, d), lambda i, j: (i, j, 0)),
            pl.BlockSpec((1, m, d), lambda i, j: (i, 0, 0)),
            const((1, d)),
            const((1, d)),
            in_hbm,
            const((CONV_WIDTH, D_CONV)),
            const((D_POOL, D_POOL)),
            const((1, D_POOL)),
            in_hbm,
            in_hbm,
            const((1, d)),
        ],
        out_specs=pl.BlockSpec((1, t, d), lambda i, j: (i, j, 0)),
        out_shape=jax.ShapeDtypeStruct((b, s, d), x.dtype),
        scratch_shapes=[
            pltpu.VMEM((d, D_IN_PROJ), _BF16),
            pltpu.VMEM((d, d), _BF16),
            pltpu.VMEM((d, 2 * D_ATT), _BF16),
            pltpu.VMEM((2, _STAGE_ROWS, D_IN_PROJ), _F32),
            pltpu.SemaphoreType.DMA((2,)),
            pltpu.VMEM((D_ATT, hm), _BF16),
            pltpu.VMEM((hm, D_ATT), _BF16),
            pltpu.VMEM((_CONV_HALO + t, D_CONV), _F32),
            pltpu.VMEM((_POOL_HALO + t, D_POOL), _F32),
            pltpu.VMEM((_POOL_HALO + t, D_POOL), _F32),
            pltpu.VMEM((_POOL_HALO + t, D_POOL), _F32),
            pltpu.VMEM((_POOL_HALO + t, 2 * _LANES), _F32),
            pltpu.VMEM((t, D_POOL), _BF16),
            pltpu.VMEM((t, D_MODEL), _BF16),
        ],
        compiler_params=pltpu.CompilerParams(
            dimension_semantics=("arbitrary", "arbitrary"),
            vmem_limit_bytes=_VMEM_LIMIT_BYTES,
        ),
        name="hybrid_main",
    )(x, mem, row(pre_w), row(mem_w), w_in, conv_w, _block_diag(pool_w).astype(_BF16),
      row(pool_scale), w_kv, w_out, row(post_w))


def kernel(x, mem, pre_norm_w, mem_norm_w, w_in, conv_w, pool_w, pool_scale, w_kv, w_out, post_norm_w):
    for l in range(pre_norm_w.shape[0]):
        x = _layer(x, mem, pre_norm_w[l], mem_norm_w[l], w_in[l], conv_w[l], pool_w[l],
                   pool_scale[l], w_kv[l], w_out[l], post_norm_w[l])
    return x
```
